```python
import math
import jax, jax.numpy as jnp
from jax import lax
import numpy as np

D_MODEL = 1024
BATCH = 8
SEQ = 4096
DEPTH = 4

HEAD_DIM = 64
N_HEADS = 8
N_KV_HEADS = 2
WINDOW = 128
ATTN_WIDTH = N_HEADS * HEAD_DIM
KV_WIDTH = N_KV_HEADS * HEAD_DIM
CONV_CH = D_MODEL // 2
CONV_WIDTH = 31
MIX_WIDTH = ATTN_WIDTH + CONV_CH
IN_WIDTH = ATTN_WIDTH + 2 * KV_WIDTH + 2 * CONV_CH
D_FF = 2816
FFN_RESIDUAL_WEIGHT = 0.5
EPS = 1e-6
NEG_INF = -1e30

kernel_name = "hybrid_swa_sink_alibi_conformer_conv_macaron"


def rms_norm(x, g):
    xf = x.astype(jnp.float32)
    y = xf * lax.rsqrt(jnp.mean(xf * xf, axis=-1, keepdims=True) + EPS)
    return (y * g.astype(jnp.float32)).astype(x.dtype)


def swiglu_ffn(h, w_in, w_out):
    gu = h @ w_in
    gate, up = jnp.split(gu, 2, axis=-1)
    return (jax.nn.silu(gate) * up) @ w_out


def alibi_slopes(n_heads):
    return jnp.exp2(-8.0 * jnp.arange(1, n_heads + 1, dtype=jnp.float32) / n_heads)


def sliding_window_sink_attention(q, k, v, sinks):
    B, S, H, hd = q.shape
    nb = S // WINDOW
    G = H // N_KV_HEADS
    qb = q.reshape(B, nb, WINDOW, N_KV_HEADS, G, hd).astype(jnp.float32)

    def band(t):
        cur = t.reshape(B, nb, WINDOW, N_KV_HEADS, hd)
        prev = jnp.pad(cur, ((0, 0), (1, 0), (0, 0), (0, 0), (0, 0)))[:, :-1]
        return jnp.concatenate([prev, cur], axis=2).astype(jnp.float32)

    kb, vb = band(k), band(v)
    scores = jnp.einsum('bnqkgd,bnskd->bkgnqs', qb, kb) * (1.0 / math.sqrt(hd))

    t_loc = jnp.arange(WINDOW)[:, None]
    s_loc = jnp.arange(2 * WINDOW)[None, :]
    dist = t_loc + WINDOW - s_loc
    in_window = (dist >= 0) & (dist < WINDOW)
    blk = jnp.arange(nb)[:, None, None]
    valid = in_window[None] & ((blk > 0) | (s_loc >= WINDOW)[None])

    slopes = alibi_slopes(H).reshape(N_KV_HEADS, G)
    bias = -slopes[:, :, None, None] * jnp.abs(dist).astype(jnp.float32)[None, None]
    scores = jnp.where(valid[None, None, None], scores + bias[:, :, None], NEG_INF)

    sink = sinks.astype(jnp.float32).reshape(N_KV_HEADS, G)[None, :, :, None, None]
    m = jnp.maximum(jnp.max(scores, axis=-1), sink)
    p = jnp.exp(scores - m[..., None])
    denom = jnp.sum(p, axis=-1) + jnp.exp(sink - m)
    p = p / denom[..., None]
    out = jnp.einsum('bkgnqs,bnskd->bnqkgd', p, vb)
    return out.reshape(B, S, H * hd).astype(q.dtype)


def conformer_conv(u, w_dw, b_dw, ln_g, ln_b):
    a, gate = jnp.split(u, 2, axis=-1)
    z = a * jax.nn.sigmoid(gate)
    C = z.shape[-1]
    y = lax.conv_general_dilated(
        z, w_dw.astype(z.dtype)[:, None, :],
        window_strides=(1,), padding=[(CONV_WIDTH - 1, 0)],
        dimension_numbers=('NWC', 'WIO', 'NWC'), feature_group_count=C)
    y = (y + b_dw).astype(jnp.float32)
    mu = jnp.mean(y, axis=-1, keepdims=True)
    var = jnp.mean(jnp.square(y - mu), axis=-1, keepdims=True)
    y = (y - mu) * lax.rsqrt(var + EPS) * ln_g.astype(jnp.float32) + ln_b.astype(jnp.float32)
    return jax.nn.silu(y).astype(u.dtype)


def setup_inputs(seed: int = 0) -> dict:
    key = jax.random.key(seed)
    ks = jax.random.split(key, 20)
    f32 = jnp.float32

    def nrm(k, shape, scale):
        return jax.random.normal(k, shape, f32) * scale

    def gain(k, shape):
        return 1.0 + 0.05 * jax.random.normal(k, shape, f32)

    return {
        "x": jax.random.normal(ks[0], (BATCH, SEQ, D_MODEL), f32),
        "norm_ffn1": gain(ks[1], (DEPTH, D_MODEL)),
        "w_ffn1_in": nrm(ks[2], (DEPTH, D_MODEL, 2 * D_FF), D_MODEL ** -0.5),
        "w_ffn1_out": nrm(ks[3], (DEPTH, D_FF, D_MODEL), D_FF ** -0.5),
        "norm_mix": gain(ks[4], (DEPTH, D_MODEL)),
        "w_in": nrm(ks[5], (DEPTH, D_MODEL, IN_WIDTH), D_MODEL ** -0.5),
        "sinks": nrm(ks[6], (DEPTH, N_HEADS), 1.0),
        "w_dw": nrm(ks[7], (DEPTH, CONV_WIDTH, CONV_CH), CONV_WIDTH ** -0.5),
        "b_dw": nrm(ks[8], (DEPTH, CONV_CH), 0.02),
        "conv_ln_g": gain(ks[9], (DEPTH, CONV_CH)),
        "conv_ln_b": nrm(ks[10], (DEPTH, CONV_CH), 0.02),
        "w_out": nrm(ks[11], (DEPTH, MIX_WIDTH, D_MODEL), MIX_WIDTH ** -0.5),
        "norm_ffn2": gain(ks[12], (DEPTH, D_MODEL)),
        "w_ffn2_in": nrm(ks[13], (DEPTH, D_MODEL, 2 * D_FF), D_MODEL ** -0.5),
        "w_ffn2_out": nrm(ks[14], (DEPTH, D_FF, D_MODEL), D_FF ** -0.5),
        "final_norm": gain(ks[15], (D_MODEL,)),
    }


def reference(x, norm_ffn1, w_ffn1_in, w_ffn1_out, norm_mix, w_in, sinks, w_dw, b_dw,
              conv_ln_g, conv_ln_b, w_out, norm_ffn2, w_ffn2_in, w_ffn2_out, final_norm):
    B, S, _ = x.shape
    split_pts = [ATTN_WIDTH, ATTN_WIDTH + KV_WIDTH, ATTN_WIDTH + 2 * KV_WIDTH]
    for l in range(DEPTH):
        h = rms_norm(x, norm_ffn1[l])
        x = x + FFN_RESIDUAL_WEIGHT * swiglu_ffn(h, w_ffn1_in[l], w_ffn1_out[l])

        h = rms_norm(x, norm_mix[l])
        proj = h @ w_in[l]
        q, k, v, u = jnp.split(proj, split_pts, axis=-1)
        attn = sliding_window_sink_attention(
            q.reshape(B, S, N_HEADS, HEAD_DIM),
            k.reshape(B, S, N_KV_HEADS, HEAD_DIM),
            v.reshape(B, S, N_KV_HEADS, HEAD_DIM),
            sinks[l])
        conv = conformer_conv(u, w_dw[l], b_dw[l], conv_ln_g[l], conv_ln_b[l])
        x = x + jnp.concatenate([attn, conv], axis=-1) @ w_out[l]

        h = rms_norm(x, norm_ffn2[l])
        x = x + FFN_RESIDUAL_WEIGHT * swiglu_ffn(h, w_ffn2_in[l], w_ffn2_out[l])
    return rms_norm(x, final_norm)
```

```python
import functools

import jax
import jax.numpy as jnp
from jax import lax
from jax.experimental import pallas as pl
from jax.experimental.pallas import tpu as pltpu

D_MODEL = 1024
DEPTH = 4
HEAD_DIM = 64
N_HEADS = 8
N_KV_HEADS = 2
WINDOW = 128
ATTN_WIDTH = N_HEADS * HEAD_DIM
KV_WIDTH = N_KV_HEADS * HEAD_DIM
CONV_CH = D_MODEL // 2
CONV_WIDTH = 31
MIX_WIDTH = ATTN_WIDTH + CONV_CH
IN_WIDTH = ATTN_WIDTH + 2 * KV_WIDTH + 2 * CONV_CH
D_FF = 2816
FFN_RESIDUAL_WEIGHT = 0.5
EPS = 1e-6
NEG_INF = -1e30

F32 = jnp.float32
BF16 = jnp.bfloat16

LANES = 128
MXU_COLS = 256
VMEM_LIMIT_BYTES = 56 * 1024 * 1024

FFN_ROWS = 512
FFN_COL_CHUNK = MXU_COLS
MIX_ROWS = 256
CONV_HALO = 32
CONV_ROW_CHUNK = 64


def _rms(x, g):
    return x * lax.rsqrt(jnp.mean(x * x, axis=-1, keepdims=True) + EPS) * g


def _sigmoid(x):
    return 1.0 / (1.0 + jnp.exp(-x))


def _ffn_body(x_ref, g_ref, win_ref, wout_ref, fg_ref, o_ref, h_ref, a_ref, *, final):
    h_ref[...] = _rms(x_ref[...], g_ref[...]).astype(BF16)
    for c in range(D_FF // FFN_COL_CHUNK):
        lo = c * FFN_COL_CHUNK
        h = h_ref[...]
        gate = jnp.dot(h, win_ref[:, lo:lo + FFN_COL_CHUNK], preferred_element_type=F32)
        up = jnp.dot(h, win_ref[:, D_FF + lo:D_FF + lo + FFN_COL_CHUNK], preferred_element_type=F32)
        a_ref[:, lo:lo + FFN_COL_CHUNK] = (gate * _sigmoid(gate) * up).astype(BF16)
    for c in range(D_MODEL // MXU_COLS):
        lo = c * MXU_COLS
        y = jnp.dot(a_ref[...], wout_ref[:, lo:lo + MXU_COLS], preferred_element_type=F32)
        o_ref[:, lo:lo + MXU_COLS] = x_ref[:, lo:lo + MXU_COLS] + FFN_RESIDUAL_WEIGHT * y
    if final:
        o_ref[...] = _rms(o_ref[...], fg_ref[...])


def _ffn_call(x2d, gain, w_in, w_out, final_gain, layer, final):
    n_rows = x2d.shape[0]
    vec_spec = pl.BlockSpec((None, 1, D_MODEL), lambda i: (layer, 0, 0))
    return pl.pallas_call(
        functools.partial(_ffn_body, final=final),
        grid=(n_rows // FFN_ROWS,),
        in_specs=[
            pl.BlockSpec((FFN_ROWS, D_MODEL), lambda i: (i, 0)),
            vec_spec,
            pl.BlockSpec((None, D_MODEL, 2 * D_FF), lambda i: (layer, 0, 0),
                         pipeline_mode=pl.Buffered(1)),
            pl.BlockSpec((None, D_FF, D_MODEL), lambda i: (layer, 0, 0),
                         pipeline_mode=pl.Buffered(1)),
            pl.BlockSpec((1, D_MODEL), lambda i: (0, 0)),
        ],
        out_specs=pl.BlockSpec((FFN_ROWS, D_MODEL), lambda i: (i, 0)),
        out_shape=jax.ShapeDtypeStruct(x2d.shape, F32),
        scratch_shapes=[
            pltpu.VMEM((FFN_ROWS, D_MODEL), BF16),
            pltpu.VMEM((FFN_ROWS, D_FF), BF16),
        ],
        compiler_params=pltpu.CompilerParams(
            dimension_semantics=("arbitrary",), vmem_limit_bytes=VMEM_LIMIT_BYTES),
        name="ffn_final" if final else "ffn",
    )(x2d, gain, w_in, w_out, final_gain)


def _mix_body(sinks_ref, x_ref, g_ref, win_ref, wdw_ref, bdw_ref, lng_ref, lnb_ref, wout_ref,
              o_ref, q_ref, k_ref, v_ref, z_ref, mix_ref, *, layer):
    first_tile = pl.program_id(1) == 0

    @pl.when(first_tile)
    def _():
        k_ref[:, 0:WINDOW, :] = jnp.zeros((4, WINDOW, KV_WIDTH), BF16)
        v_ref[:, 0:WINDOW, :] = jnp.zeros((4, WINDOW, KV_WIDTH), BF16)
        z_ref[0:CONV_HALO, :] = jnp.zeros((CONV_HALO, CONV_CH), F32)

    h = _rms(x_ref[...], g_ref[...]).astype(BF16)

    q = jnp.dot(h, win_ref[:, 0:ATTN_WIDTH], preferred_element_type=F32)
    q_ref[...] = (q * (HEAD_DIM ** -0.5)).astype(BF16)
    kv = jnp.dot(h, win_ref[:, ATTN_WIDTH:ATTN_WIDTH + 2 * KV_WIDTH], preferred_element_type=F32)
    lane = lax.broadcasted_iota(jnp.int32, (MIX_ROWS, KV_WIDTH), 1)
    low = lane < HEAD_DIM
    for t, dst in ((kv[:, 0:KV_WIDTH], k_ref), (kv[:, KV_WIDTH:2 * KV_WIDTH], v_ref)):
        t_sw = pltpu.roll(t, HEAD_DIM, axis=1)
        zero = jnp.zeros_like(t)
        dst[0, WINDOW:, :] = jnp.where(low, t, zero).astype(BF16)
        dst[1, WINDOW:, :] = jnp.where(low, zero, t_sw).astype(BF16)
        dst[2, WINDOW:, :] = jnp.where(low, t_sw, zero).astype(BF16)
        dst[3, WINDOW:, :] = jnp.where(low, zero, t).astype(BF16)

    u_lo = ATTN_WIDTH + 2 * KV_WIDTH
    a = jnp.dot(h, win_ref[:, u_lo:u_lo + CONV_CH], preferred_element_type=F32)
    gate = jnp.dot(h, win_ref[:, u_lo + CONV_CH:u_lo + 2 * CONV_CH], preferred_element_type=F32)
    z_ref[CONV_HALO:, :] = a * _sigmoid(gate)

    t_loc = lax.broadcasted_iota(jnp.int32, (WINDOW, 2 * WINDOW), 0)
    s_loc = lax.broadcasted_iota(jnp.int32, (WINDOW, 2 * WINDOW), 1)
    dist = t_loc + WINDOW - s_loc
    in_window = (dist >= 0) & (dist < WINDOW)
    dist_f = jnp.abs(dist).astype(F32)
    lane_o = lax.broadcasted_iota(jnp.int32, (2 * WINDOW, 2 * HEAD_DIM), 1)
    first_key = jnp.where(first_tile, WINDOW, 0)
    for n in range(MIX_ROWS // WINDOW):
        r0 = n * WINDOW
        valid = (in_window & (s_loc >= first_key)) if n == 0 else in_window
        for kvh in range(N_KV_HEADS):
            c0 = kvh * 4 * HEAD_DIM
            qg = jnp.concatenate([q_ref[r0:r0 + WINDOW, c0:c0 + 2 * HEAD_DIM],
                                  q_ref[r0:r0 + WINDOW, c0 + 2 * HEAD_DIM:c0 + 4 * HEAD_DIM]], axis=0)
            probs, inv = [], []
            for p in range(2):
                kvar = k_ref[2 * kvh + p, r0:r0 + 2 * WINDOW, :]
                s = lax.dot_general(qg, kvar, (((1,), (1,)), ((), ())), preferred_element_type=F32)
                p_halves, inv_halves = [], []
                for half in range(2):
                    head = 4 * kvh + 2 * half + p
                    slope = 2.0 ** (-8.0 * (head + 1) / N_HEADS)
                    sink = sinks_ref[layer, head]
                    sh = s[half * WINDOW:(half + 1) * WINDOW, :]
                    sh = jnp.where(valid, sh + (-slope) * dist_f, NEG_INF)
                    m = jnp.maximum(jnp.max(sh, axis=-1, keepdims=True), sink)
                    e = jnp.exp(sh - m)
                    denom = jnp.sum(e, axis=-1, keepdims=True) + jnp.exp(sink - m)
                    p_halves.append(e.astype(BF16))
                    inv_halves.append(1.0 / denom)
                probs.append(jnp.concatenate(p_halves, axis=0))
                inv.append(jnp.concatenate(inv_halves, axis=0))
            pcat = jnp.concatenate(probs, axis=1)
            vcat = jnp.concatenate([v_ref[2 * kvh, r0:r0 + 2 * WINDOW, :],
                                    v_ref[2 * kvh + 1, r0:r0 + 2 * WINDOW, :]], axis=0)
            o = jnp.dot(pcat, vcat, preferred_element_type=F32)
            o = o * jnp.where(lane_o < HEAD_DIM, inv[0], inv[1])
            mix_ref[r0:r0 + WINDOW, c0:c0 + 2 * HEAD_DIM] = o[0:WINDOW].astype(BF16)
            mix_ref[r0:r0 + WINDOW, c0 + 2 * HEAD_DIM:c0 + 4 * HEAD_DIM] = o[WINDOW:].astype(BF16)

    b_dw = bdw_ref[...]
    ln_g = lng_ref[...]
    ln_b = lnb_ref[...]
    tap0 = CONV_HALO - (CONV_WIDTH - 1)
    for r in range(MIX_ROWS // CONV_ROW_CHUNK):
        r0 = r * CONV_ROW_CHUNK
        acc = None
        for j in range(CONV_WIDTH):
            term = wdw_ref[j:j + 1, :] * z_ref[r0 + tap0 + j:r0 + tap0 + j + CONV_ROW_CHUNK, :]
            acc = term if acc is None else acc + term
        y = acc + b_dw
        mu = jnp.mean(y, axis=-1, keepdims=True)
        var = jnp.mean(jnp.square(y - mu), axis=-1, keepdims=True)
        y = (y - mu) * lax.rsqrt(var + EPS) * ln_g + ln_b
        mix_ref[r0:r0 + CONV_ROW_CHUNK, ATTN_WIDTH:] = (y * _sigmoid(y)).astype(BF16)

    k_ref[:, 0:WINDOW, :] = k_ref[:, MIX_ROWS:MIX_ROWS + WINDOW, :]
    v_ref[:, 0:WINDOW, :] = v_ref[:, MIX_ROWS:MIX_ROWS + WINDOW, :]
    z_ref[0:CONV_HALO, :] = z_ref[MIX_ROWS:MIX_ROWS + CONV_HALO, :]

    o_ref[...] = x_ref[...] + jnp.dot(mix_ref[...], wout_ref[...], preferred_element_type=F32)


def _mix_call(x, sinks, gain, w_in, w_dw, b_dw, ln_g, ln_b, w_out, layer):
    batch, seq, _ = x.shape
    smem_spec = pl.BlockSpec(memory_space=pltpu.SMEM)

    def vec_spec(width):
        return pl.BlockSpec((None, 1, width), lambda b, i: (layer, 0, 0))

    x_spec = pl.BlockSpec((None, MIX_ROWS, D_MODEL), lambda b, i: (b, i, 0))
    return pl.pallas_call(
        functools.partial(_mix_body, layer=layer),
        grid=(batch, seq // MIX_ROWS),
        in_specs=[
            smem_spec,
            x_spec,
            vec_spec(D_MODEL),
            pl.BlockSpec((None, D_MODEL, IN_WIDTH), lambda b, i: (layer, 0, 0)),
            pl.BlockSpec((None, CONV_WIDTH, CONV_CH), lambda b, i: (layer, 0, 0)),
            vec_spec(CONV_CH),
            vec_spec(CONV_CH),
            vec_spec(CONV_CH),
            pl.BlockSpec((None, MIX_WIDTH, D_MODEL), lambda b, i: (layer, 0, 0)),
        ],
        out_specs=x_spec,
        out_shape=jax.ShapeDtypeStruct(x.shape, F32),
        scratch_shapes=[
            pltpu.VMEM((MIX_ROWS, ATTN_WIDTH), BF16),
            pltpu.VMEM((4, WINDOW + MIX_ROWS, KV_WIDTH), BF16),
            pltpu.VMEM((4, WINDOW + MIX_ROWS, KV_WIDTH), BF16),
            pltpu.VMEM((CONV_HALO + MIX_ROWS, CONV_CH), F32),
            pltpu.VMEM((MIX_ROWS, MIX_WIDTH), BF16),
        ],
        compiler_params=pltpu.CompilerParams(
            dimension_semantics=("arbitrary", "arbitrary"), vmem_limit_bytes=VMEM_LIMIT_BYTES),
        name="mix",
    )(sinks, x, gain, w_in, w_dw, b_dw, ln_g, ln_b, w_out)


def kernel(x, norm_ffn1, w_ffn1_in, w_ffn1_out, norm_mix, w_in, sinks, w_dw, b_dw, conv_ln_g,
           conv_ln_b, w_out, norm_ffn2, w_ffn2_in, w_ffn2_out, final_norm):
    batch, seq, d = x.shape
    assert (d, seq % MIX_ROWS, (batch * seq) % FFN_ROWS) == (D_MODEL, 0, 0)

    def row3(v):
        return v.reshape(v.shape[0], 1, v.shape[1])

    w1i, w1o = w_ffn1_in.astype(BF16), w_ffn1_out.astype(BF16)
    w2i, w2o = w_ffn2_in.astype(BF16), w_ffn2_out.astype(BF16)
    wi, wo = w_in.astype(BF16), w_out.astype(BF16)
    g1, gm, g2 = row3(norm_ffn1), row3(norm_mix), row3(norm_ffn2)
    bdw, lng, lnb = row3(b_dw), row3(conv_ln_g), row3(conv_ln_b)
    fg = final_norm.reshape(1, D_MODEL)

    for l in range(DEPTH):
        x = _ffn_call(x.reshape(batch * seq, d), g1, w1i, w1o, fg, l, False).reshape(batch, seq, d)
        x = _mix_call(x, sinks, gm, wi, w_dw, bdw, lng, lnb, wo, l)
        x = _ffn_call(x.reshape(batch * seq, d), g2, w2i, w2o, fg, l, l == DEPTH - 1).reshape(batch, seq, d)
    return x
```

```python
import functools
import math

import jax
import jax.numpy as jnp
from jax import lax
from jax.experimental import pallas as pl
from jax.experimental.pallas import tpu as pltpu

D_MODEL = 1024
DEPTH = 4
HEAD_DIM = 64
N_HEADS = 8
N_KV_HEADS = 2
WINDOW = 128
ATTN_WIDTH = N_HEADS * HEAD_DIM
KV_WIDTH = N_KV_HEADS * HEAD_DIM
CONV_CH = D_MODEL // 2
CONV_WIDTH = 31
MIX_WIDTH = ATTN_WIDTH + CONV_CH
IN_WIDTH = ATTN_WIDTH + 2 * KV_WIDTH + 2 * CONV_CH
D_FF = 2816
FFN_RESIDUAL_WEIGHT = 0.5
EPS = 1e-6
NEG_INF = -1e30
LOG2E = math.log2(math.e)

F32 = jnp.float32
BF16 = jnp.bfloat16

LANES = 128
SUBLANES = 8
MXU_COLS = 256
VMEM_LIMIT_BYTES = 56 * 1024 * 1024

FFN_ROWS = 512
FFN_COL_CHUNK = MXU_COLS
MIX_ROWS = 256
CONV_HALO = 32
CONV_ROW_CHUNK = 64
Z_CHUNKS = CONV_CH // LANES


def _rms(x, g):
    return x * lax.rsqrt(jnp.mean(x * x, axis=-1, keepdims=True) + EPS) * g


def _sigmoid(x):
    return 1.0 / (1.0 + jnp.exp(-x))


def _ffn_body(x_ref, g_ref, win_ref, wout_ref, fg_ref, o_ref, h_ref, a_ref, *, final):
    h_ref[...] = _rms(x_ref[...], g_ref[...]).astype(BF16)
    for c in range(D_FF // FFN_COL_CHUNK):
        lo = c * FFN_COL_CHUNK
        h = h_ref[...]
        gate = jnp.dot(h, win_ref[:, lo:lo + FFN_COL_CHUNK], preferred_element_type=F32)
        up = jnp.dot(h, win_ref[:, D_FF + lo:D_FF + lo + FFN_COL_CHUNK], preferred_element_type=F32)
        a_ref[:, lo:lo + FFN_COL_CHUNK] = (gate * _sigmoid(gate) * up).astype(BF16)
    for c in range(D_MODEL // MXU_COLS):
        lo = c * MXU_COLS
        y = jnp.dot(a_ref[...], wout_ref[:, lo:lo + MXU_COLS], preferred_element_type=F32)
        o_ref[:, lo:lo + MXU_COLS] = x_ref[:, lo:lo + MXU_COLS] + FFN_RESIDUAL_WEIGHT * y
    if final:
        o_ref[...] = _rms(o_ref[...], fg_ref[...])


def _ffn_call(x2d, gain, w_in, w_out, final_gain, layer, final):
    n_rows = x2d.shape[0]
    vec_spec = pl.BlockSpec((None, 1, D_MODEL), lambda i: (layer, 0, 0))
    return pl.pallas_call(
        functools.partial(_ffn_body, final=final),
        grid=(n_rows // FFN_ROWS,),
        in_specs=[
            pl.BlockSpec((FFN_ROWS, D_MODEL), lambda i: (i, 0)),
            vec_spec,
            pl.BlockSpec((None, D_MODEL, 2 * D_FF), lambda i: (layer, 0, 0),
                         pipeline_mode=pl.Buffered(1)),
            pl.BlockSpec((None, D_FF, D_MODEL), lambda i: (layer, 0, 0),
                         pipeline_mode=pl.Buffered(1)),
            pl.BlockSpec((1, D_MODEL), lambda i: (0, 0)),
        ],
        out_specs=pl.BlockSpec((FFN_ROWS, D_MODEL), lambda i: (i, 0)),
        out_shape=jax.ShapeDtypeStruct(x2d.shape, F32),
        scratch_shapes=[
            pltpu.VMEM((FFN_ROWS, D_MODEL), BF16),
            pltpu.VMEM((FFN_ROWS, D_FF), BF16),
        ],
        compiler_params=pltpu.CompilerParams(
            dimension_semantics=("arbitrary",), vmem_limit_bytes=VMEM_LIMIT_BYTES),
        name="ffn_final" if final else "ffn",
    )(x2d, gain, w_in, w_out, final_gain)


def _mix_body(sinks_ref, x_ref, g_ref, win_ref, wdw_ref, bdw_ref, lng_ref, lnb_ref, wout_ref,
              o_ref, q_ref, k_ref, v_ref, z_ref, mix_ref, bias_ref, *, layer):
    first_tile = pl.program_id(1) == 0
    t_loc = lax.broadcasted_iota(jnp.int32, (WINDOW, 2 * WINDOW), 0)
    s_loc = lax.broadcasted_iota(jnp.int32, (WINDOW, 2 * WINDOW), 1)
    dist = t_loc + WINDOW - s_loc
    in_window = (dist >= 0) & (dist < WINDOW)

    @pl.when(first_tile & (pl.program_id(0) == 0))
    def _():
        dist_f = jnp.abs(dist).astype(F32)
        for head in range(N_HEADS):
            slope = 2.0 ** (-8.0 * (head + 1) / N_HEADS)
            bias_ref[head] = (-slope * LOG2E) * dist_f

    @pl.when(first_tile)
    def _():
        k_ref[:, 0:WINDOW, :] = jnp.zeros((4, WINDOW, KV_WIDTH), BF16)
        v_ref[:, 0:WINDOW, :] = jnp.zeros((4, WINDOW, KV_WIDTH), BF16)
        z_ref[0:Z_CHUNKS * CONV_HALO, :] = jnp.zeros((Z_CHUNKS * CONV_HALO, LANES), F32)

    h = _rms(x_ref[...], g_ref[...]).astype(BF16)

    q = jnp.dot(h, win_ref[:, 0:ATTN_WIDTH], preferred_element_type=F32)
    q_ref[...] = (q * (LOG2E * HEAD_DIM ** -0.5)).astype(BF16)
    kv = jnp.dot(h, win_ref[:, ATTN_WIDTH:ATTN_WIDTH + 2 * KV_WIDTH], preferred_element_type=F32)
    lane = lax.broadcasted_iota(jnp.int32, (MIX_ROWS, KV_WIDTH), 1)
    low = lane < HEAD_DIM
    for t, dst in ((kv[:, 0:KV_WIDTH], k_ref), (kv[:, KV_WIDTH:2 * KV_WIDTH], v_ref)):
        t_sw = pltpu.roll(t, HEAD_DIM, axis=1)
        zero = jnp.zeros_like(t)
        dst[0, WINDOW:, :] = jnp.where(low, t, zero).astype(BF16)
        dst[1, WINDOW:, :] = jnp.where(low, zero, t_sw).astype(BF16)
        dst[2, WINDOW:, :] = jnp.where(low, t_sw, zero).astype(BF16)
        dst[3, WINDOW:, :] = jnp.where(low, zero, t).astype(BF16)

    u_lo = ATTN_WIDTH + 2 * KV_WIDTH
    a = jnp.dot(h, win_ref[:, u_lo:u_lo + CONV_CH], preferred_element_type=F32)
    gate = jnp.dot(h, win_ref[:, u_lo + CONV_CH:u_lo + 2 * CONV_CH], preferred_element_type=F32)
    z = a * _sigmoid(gate)
    for i in range(MIX_ROWS // SUBLANES):
        for c in range(Z_CHUNKS):
            row = Z_CHUNKS * (CONV_HALO + SUBLANES * i) + c
            z_ref[pl.ds(row, SUBLANES, stride=Z_CHUNKS), :] = (
                z[SUBLANES * i:SUBLANES * (i + 1), LANES * c:LANES * (c + 1)])

    lane_o = lax.broadcasted_iota(jnp.int32, (2 * WINDOW, 2 * HEAD_DIM), 1)
    first_key = jnp.where(first_tile, WINDOW, 0)
    for n in range(MIX_ROWS // WINDOW):
        r0 = n * WINDOW
        valid = (in_window & (s_loc >= first_key)) if n == 0 else in_window
        for kvh in range(N_KV_HEADS):
            c0 = kvh * 4 * HEAD_DIM
            qg = jnp.concatenate([q_ref[r0:r0 + WINDOW, c0:c0 + 2 * HEAD_DIM],
                                  q_ref[r0:r0 + WINDOW, c0 + 2 * HEAD_DIM:c0 + 4 * HEAD_DIM]], axis=0)
            probs, inv = [], []
            for p in range(2):
                kvar = k_ref[2 * kvh + p, r0:r0 + 2 * WINDOW, :]
                s = lax.dot_general(qg, kvar, (((1,), (1,)), ((), ())), preferred_element_type=F32)
                p_halves, inv_halves = [], []
                for half in range(2):
                    head = 4 * kvh + 2 * half + p
                    sink = sinks_ref[layer, head] * LOG2E
                    sh = s[half * WINDOW:(half + 1) * WINDOW, :]
                    sh = jnp.where(valid, sh + bias_ref[head], NEG_INF)
                    m = jnp.maximum(jnp.max(sh, axis=-1, keepdims=True), sink)
                    e = jnp.exp2(sh - m)
                    denom = jnp.sum(e, axis=-1, keepdims=True) + jnp.exp2(sink - m)
                    p_halves.append(e.astype(BF16))
                    inv_halves.append(1.0 / denom)
                probs.append(jnp.concatenate(p_halves, axis=0))
                inv.append(jnp.concatenate(inv_halves, axis=0))
            pcat = jnp.concatenate(probs, axis=1)
            vcat = jnp.concatenate([v_ref[2 * kvh, r0:r0 + 2 * WINDOW, :],
                                    v_ref[2 * kvh + 1, r0:r0 + 2 * WINDOW, :]], axis=0)
            o = jnp.dot(pcat, vcat, preferred_element_type=F32)
            o = o * jnp.where(lane_o < HEAD_DIM, inv[0], inv[1])
            mix_ref[r0:r0 + WINDOW, c0:c0 + 2 * HEAD_DIM] = o[0:WINDOW].astype(BF16)
            mix_ref[r0:r0 + WINDOW, c0 + 2 * HEAD_DIM:c0 + 4 * HEAD_DIM] = o[WINDOW:].astype(BF16)

    b_dw = bdw_ref[...]
    ln_g = lng_ref[...]
    ln_b = lnb_ref[...]
    tap0 = CONV_HALO - (CONV_WIDTH - 1)
    tiles = CONV_ROW_CHUNK // SUBLANES
    for r in range(MIX_ROWS // CONV_ROW_CHUNK):
        r0 = r * CONV_ROW_CHUNK
        cols = []
        for c in range(Z_CHUNKS):
            acc = [None] * tiles
            for j in range(CONV_WIDTH):
                w = wdw_ref[j:j + 1, LANES * c:LANES * (c + 1)]
                for g in range(tiles):
                    row = Z_CHUNKS * (r0 + SUBLANES * g + tap0 + j) + c
                    term = w * z_ref[pl.ds(row, SUBLANES, stride=Z_CHUNKS), :]
                    acc[g] = term if acc[g] is None else acc[g] + term
            cols.append(jnp.concatenate(acc, axis=0))
        y = jnp.concatenate(cols, axis=1) + b_dw
        mu = jnp.mean(y, axis=-1, keepdims=True)
        var = jnp.mean(jnp.square(y - mu), axis=-1, keepdims=True)
        y = (y - mu) * lax.rsqrt(var + EPS) * ln_g + ln_b
        mix_ref[r0:r0 + CONV_ROW_CHUNK, ATTN_WIDTH:] = (y * _sigmoid(y)).astype(BF16)

    k_ref[:, 0:WINDOW, :] = k_ref[:, MIX_ROWS:MIX_ROWS + WINDOW, :]
    v_ref[:, 0:WINDOW, :] = v_ref[:, MIX_ROWS:MIX_ROWS + WINDOW, :]
    z_ref[0:Z_CHUNKS * CONV_HALO, :] = z_ref[Z_CHUNKS * MIX_ROWS:Z_CHUNKS * (MIX_ROWS + CONV_HALO), :]

    o_ref[...] = x_ref[...] + jnp.dot(mix_ref[...], wout_ref[...], preferred_element_type=F32)


def _mix_call(x, sinks, gain, w_in, w_dw, b_dw, ln_g, ln_b, w_out, layer):
    batch, seq, _ = x.shape
    smem_spec = pl.BlockSpec(memory_space=pltpu.SMEM)

    def vec_spec(width):
        return pl.BlockSpec((None, 1, width), lambda b, i: (layer, 0, 0))

    x_spec = pl.BlockSpec((None, MIX_ROWS, D_MODEL), lambda b, i: (b, i, 0))
    return pl.pallas_call(
        functools.partial(_mix_body, layer=layer),
        grid=(batch, seq // MIX_ROWS),
        in_specs=[
            smem_spec,
            x_spec,
            vec_spec(D_MODEL),
            pl.BlockSpec((None, D_MODEL, IN_WIDTH), lambda b, i: (layer, 0, 0)),
            pl.BlockSpec((None, CONV_WIDTH, CONV_CH), lambda b, i: (layer, 0, 0)),
            vec_spec(CONV_CH),
            vec_spec(CONV_CH),
            vec_spec(CONV_CH),
            pl.BlockSpec((None, MIX_WIDTH, D_MODEL), lambda b, i: (layer, 0, 0)),
        ],
        out_specs=x_spec,
        out_shape=jax.ShapeDtypeStruct(x.shape, F32),
        scratch_shapes=[
            pltpu.VMEM((MIX_ROWS, ATTN_WIDTH), BF16),
            pltpu.VMEM((4, WINDOW + MIX_ROWS, KV_WIDTH), BF16),
            pltpu.VMEM((4, WINDOW + MIX_ROWS, KV_WIDTH), BF16),
            pltpu.VMEM((Z_CHUNKS * (CONV_HALO + MIX_ROWS), LANES), F32),
            pltpu.VMEM((MIX_ROWS, MIX_WIDTH), BF16),
            pltpu.VMEM((N_HEADS, WINDOW, 2 * WINDOW), F32),
        ],
        compiler_params=pltpu.CompilerParams(
            dimension_semantics=("arbitrary", "arbitrary"), vmem_limit_bytes=VMEM_LIMIT_BYTES),
        name="mix",
    )(sinks, x, gain, w_in, w_dw, b_dw, ln_g, ln_b, w_out)


def kernel(x, norm_ffn1, w_ffn1_in, w_ffn1_out, norm_mix, w_in, sinks, w_dw, b_dw, conv_ln_g,
           conv_ln_b, w_out, norm_ffn2, w_ffn2_in, w_ffn2_out, final_norm):
    batch, seq, d = x.shape
    assert (d, seq % MIX_ROWS, (batch * seq) % FFN_ROWS) == (D_MODEL, 0, 0)

    def row3(v):
        return v.reshape(v.shape[0], 1, v.shape[1])

    w1i, w1o = w_ffn1_in.astype(BF16), w_ffn1_out.astype(BF16)
    w2i, w2o = w_ffn2_in.astype(BF16), w_ffn2_out.astype(BF16)
    wi, wo = w_in.astype(BF16), w_out.astype(BF16)
    g1, gm, g2 = row3(norm_ffn1), row3(norm_mix), row3(norm_ffn2)
    bdw, lng, lnb = row3(b_dw), row3(conv_ln_g), row3(conv_ln_b)
    fg = final_norm.reshape(1, D_MODEL)

    for l in range(DEPTH):
        x = _ffn_call(x.reshape(batch * seq, d), g1, w1i, w1o, fg, l, False).reshape(batch, seq, d)
        x = _mix_call(x, sinks, gm, wi, w_dw, bdw, lng, lnb, wo, l)
        x = _ffn_call(x.reshape(batch * seq, d), g2, w2i, w2o, fg, l, l == DEPTH - 1).reshape(batch, seq, d)
    return x
```

```python
import functools
import math

import jax
import jax.numpy as jnp
from jax import lax
from jax.experimental import pallas as pl
from jax.experimental.pallas import tpu as pltpu

D_MODEL = 1024
DEPTH = 4
HEAD_DIM = 64
N_HEADS = 8
N_KV_HEADS = 2
WINDOW = 128
ATTN_WIDTH = N_HEADS * HEAD_DIM
KV_WIDTH = N_KV_HEADS * HEAD_DIM
CONV_CH = D_MODEL // 2
CONV_WIDTH = 31
MIX_WIDTH = ATTN_WIDTH + CONV_CH
IN_WIDTH = ATTN_WIDTH + 2 * KV_WIDTH + 2 * CONV_CH
D_FF = 2816
FFN_RESIDUAL_WEIGHT = 0.5
EPS = 1e-6
NEG_INF = -1e30
LOG2E = math.log2(math.e)

F32 = jnp.float32
BF16 = jnp.bfloat16

LANES = 128
SUBLANES = 8
MXU_COLS = 256
VMEM_LIMIT_BYTES = 56 * 1024 * 1024

FFN_ROWS = 512
FFN_COL_CHUNK = MXU_COLS
MIX_ROWS = 256
CONV_HALO = 32
CONV_ROW_CHUNK = 64
CONV_PIECE_ROWS = 32
Z_CHUNKS = CONV_CH // LANES
Z_HALO_ROWS = Z_CHUNKS * CONV_HALO
Z_BODY_ROWS = Z_CHUNKS * MIX_ROWS


def _rms(x, g):
    return x * lax.rsqrt(jnp.mean(x * x, axis=-1, keepdims=True) + EPS) * g


def _sigmoid(x):
    return 1.0 / (1.0 + jnp.exp(-x))


def _ffn_body(x_ref, g_ref, win_ref, wout_ref, fg_ref, o_ref, h_ref, a_ref, *, final):
    h_ref[...] = _rms(x_ref[...], g_ref[...]).astype(BF16)
    for c in range(D_FF // FFN_COL_CHUNK):
        lo = c * FFN_COL_CHUNK
        h = h_ref[...]
        gate = jnp.dot(h, win_ref[:, lo:lo + FFN_COL_CHUNK], preferred_element_type=F32)
        up = jnp.dot(h, win_ref[:, D_FF + lo:D_FF + lo + FFN_COL_CHUNK], preferred_element_type=F32)
        a_ref[:, lo:lo + FFN_COL_CHUNK] = (gate * _sigmoid(gate) * up).astype(BF16)
    for c in range(D_MODEL // MXU_COLS):
        lo = c * MXU_COLS
        y = jnp.dot(a_ref[...], wout_ref[:, lo:lo + MXU_COLS], preferred_element_type=F32)
        o_ref[:, lo:lo + MXU_COLS] = x_ref[:, lo:lo + MXU_COLS] + FFN_RESIDUAL_WEIGHT * y
    if final:
        o_ref[...] = _rms(o_ref[...], fg_ref[...])


def _ffn_call(x2d, gain, w_in, w_out, final_gain, layer, final):
    n_rows = x2d.shape[0]
    vec_spec = pl.BlockSpec((None, 1, D_MODEL), lambda i: (layer, 0, 0))
    return pl.pallas_call(
        functools.partial(_ffn_body, final=final),
        grid=(n_rows // FFN_ROWS,),
        in_specs=[
            pl.BlockSpec((FFN_ROWS, D_MODEL), lambda i: (i, 0)),
            vec_spec,
            pl.BlockSpec((None, D_MODEL, 2 * D_FF), lambda i: (layer, 0, 0),
                         pipeline_mode=pl.Buffered(1)),
            pl.BlockSpec((None, D_FF, D_MODEL), lambda i: (layer, 0, 0),
                         pipeline_mode=pl.Buffered(1)),
            pl.BlockSpec((1, D_MODEL), lambda i: (0, 0)),
        ],
        out_specs=pl.BlockSpec((FFN_ROWS, D_MODEL), lambda i: (i, 0)),
        out_shape=jax.ShapeDtypeStruct(x2d.shape, F32),
        scratch_shapes=[
            pltpu.VMEM((FFN_ROWS, D_MODEL), BF16),
            pltpu.VMEM((FFN_ROWS, D_FF), BF16),
        ],
        compiler_params=pltpu.CompilerParams(
            dimension_semantics=("arbitrary",), vmem_limit_bytes=VMEM_LIMIT_BYTES),
        name="ffn_final" if final else "ffn",
    )(x2d, gain, w_in, w_out, final_gain)


def _interleave(threads):
    gens = [g for g, _ in threads]
    total = [n for _, n in threads]
    done = [0] * len(gens)
    while True:
        cand = [j for j in range(len(gens)) if done[j] < total[j]]
        if not cand:
            break
        i = min(cand, key=lambda j: (done[j] / total[j], j))
        next(gens[i])
        done[i] += 1


def _stage1(x_ref, r0, g_ref, win_ref, h_ref, raw_ref, dst, src, first):
    q_dst, k_dst, v_dst, z_dst = dst
    _, k_src, v_src, z_src = src

    def halo(tail):
        return tail if first is False else jnp.where(first, jnp.zeros_like(tail), tail)

    for d, s in ((k_dst, k_src), (v_dst, v_src)):
        d[:, 0:WINDOW, :] = halo(s[:, MIX_ROWS:MIX_ROWS + WINDOW, :])
    z_dst[0:Z_HALO_ROWS, :] = halo(z_src[Z_BODY_ROWS:Z_BODY_ROWS + Z_HALO_ROWS, :])
    h_ref[...] = _rms(x_ref[r0:r0 + MIX_ROWS, :], g_ref[...]).astype(BF16)
    yield
    for c in range(IN_WIDTH // MXU_COLS):
        lo = c * MXU_COLS
        raw_ref[:, lo:lo + MXU_COLS] = jnp.dot(h_ref[...], win_ref[:, lo:lo + MXU_COLS],
                                               preferred_element_type=F32)
        yield
    q_dst[...] = (raw_ref[:, 0:ATTN_WIDTH] * (LOG2E * HEAD_DIM ** -0.5)).astype(BF16)
    lane = lax.broadcasted_iota(jnp.int32, (MIX_ROWS, KV_WIDTH), 1)
    low = lane < HEAD_DIM
    for off, d in ((ATTN_WIDTH, k_dst), (ATTN_WIDTH + KV_WIDTH, v_dst)):
        t = raw_ref[:, off:off + KV_WIDTH]
        t_sw = pltpu.roll(t, HEAD_DIM, axis=1)
        zero = jnp.zeros_like(t)
        d[0, WINDOW:, :] = jnp.where(low, t, zero).astype(BF16)
        d[1, WINDOW:, :] = jnp.where(low, zero, t_sw).astype(BF16)
        d[2, WINDOW:, :] = jnp.where(low, t_sw, zero).astype(BF16)
        d[3, WINDOW:, :] = jnp.where(low, zero, t).astype(BF16)
    yield
    u_lo = ATTN_WIDTH + 2 * KV_WIDTH
    for c in range(Z_CHUNKS):
        lo = u_lo + c * LANES
        z = raw_ref[:, lo:lo + LANES] * _sigmoid(raw_ref[:, lo + CONV_CH:lo + CONV_CH + LANES])
        for i in range(MIX_ROWS // SUBLANES):
            row = Z_HALO_ROWS + Z_CHUNKS * SUBLANES * i + c
            z_dst[pl.ds(row, SUBLANES, stride=Z_CHUNKS), :] = z[SUBLANES * i:SUBLANES * (i + 1), :]
        if c % 2 == 1:
            yield


def _stage2_attention(sinks_ref, src, mix_ref, s_ref, p_ref, bias_ref, mask_ref, n, first, *, layer):
    q_ref, k_ref, v_ref, _ = src
    r0 = n * WINDOW
    mask_idx = first if n == 0 else 0
    for kvh in range(N_KV_HEADS):
        c0 = kvh * 4 * HEAD_DIM
        slot = 2 * n + kvh
        qg = jnp.concatenate([q_ref[r0:r0 + WINDOW, c0:c0 + 2 * HEAD_DIM],
                              q_ref[r0:r0 + WINDOW, c0 + 2 * HEAD_DIM:c0 + 4 * HEAD_DIM]], axis=0)
        for p in range(2):
            s_ref[slot, :, 2 * WINDOW * p:2 * WINDOW * (p + 1)] = lax.dot_general(
                qg, k_ref[2 * kvh + p, r0:r0 + 2 * WINDOW, :],
                (((1,), (1,)), ((), ())), preferred_element_type=F32)
        yield
        for p in range(2):
            valid = mask_ref[mask_idx] > 0.5
            for half in range(2):
                head = 4 * kvh + 2 * half + p
                sink = sinks_ref[layer, head] * LOG2E
                rows = slice(half * WINDOW, (half + 1) * WINDOW)
                cols = slice(2 * WINDOW * p, 2 * WINDOW * (p + 1))
                sh = jnp.where(valid, s_ref[slot, rows, cols] + bias_ref[head], NEG_INF)
                m = jnp.maximum(jnp.max(sh, axis=-1, keepdims=True), sink)
                e = jnp.exp2(sh - m)
                denom = jnp.sum(e, axis=-1, keepdims=True) + jnp.exp2(sink - m)
                p_ref[slot, rows, cols] = (e * (1.0 / denom)).astype(BF16)
            yield
        vcat = jnp.concatenate([v_ref[2 * kvh, r0:r0 + 2 * WINDOW, :],
                                v_ref[2 * kvh + 1, r0:r0 + 2 * WINDOW, :]], axis=0)
        o = jnp.dot(p_ref[slot], vcat, preferred_element_type=F32)
        mix_ref[r0:r0 + WINDOW, c0:c0 + 2 * HEAD_DIM] = o[0:WINDOW].astype(BF16)
        mix_ref[r0:r0 + WINDOW, c0 + 2 * HEAD_DIM:c0 + 4 * HEAD_DIM] = o[WINDOW:].astype(BF16)
        yield


def _stage2_conv(wdw_ref, bdw_ref, lng_ref, lnb_ref, src, y_ref, mix_ref):
    z_ref = src[3]
    tap0 = CONV_HALO - (CONV_WIDTH - 1)
    for r in range(MIX_ROWS // CONV_ROW_CHUNK):
        r0 = r * CONV_ROW_CHUNK
        for part in range(CONV_ROW_CHUNK // CONV_PIECE_ROWS):
            p0 = part * CONV_PIECE_ROWS
            tiles = CONV_PIECE_ROWS // SUBLANES
            acc = [[None] * Z_CHUNKS for _ in range(tiles)]
            for j in range(CONV_WIDTH):
                w = [wdw_ref[j:j + 1, LANES * c:LANES * (c + 1)] for c in range(Z_CHUNKS)]
                for g in range(tiles):
                    for c in range(Z_CHUNKS):
                        row = Z_CHUNKS * (r0 + p0 + SUBLANES * g + tap0 + j) + c
                        term = w[c] * z_ref[pl.ds(row, SUBLANES, stride=Z_CHUNKS), :]
                        acc[g][c] = term if acc[g][c] is None else acc[g][c] + term
            y_ref[p0:p0 + CONV_PIECE_ROWS, :] = jnp.concatenate(
                [jnp.concatenate(a, axis=1) for a in acc], axis=0)
            yield
        y = y_ref[...] + bdw_ref[...]
        mu = jnp.mean(y, axis=-1, keepdims=True)
        var = jnp.mean(jnp.square(y - mu), axis=-1, keepdims=True)
        y = (y - mu) * lax.rsqrt(var + EPS) * lng_ref[...] + lnb_ref[...]
        mix_ref[r0:r0 + CONV_ROW_CHUNK, ATTN_WIDTH:] = (y * _sigmoid(y)).astype(BF16)
        yield


def _stage3(xres_ref, r0, wout_ref, mix_ref, o_ref):
    for c in range(D_MODEL // MXU_COLS):
        lo = c * MXU_COLS
        y = jnp.dot(mix_ref[...], wout_ref[:, lo:lo + MXU_COLS], preferred_element_type=F32)
        o_ref[r0:r0 + MIX_ROWS, lo:lo + MXU_COLS] = xres_ref[r0:r0 + MIX_ROWS, lo:lo + MXU_COLS] + y
        yield


def _mix_body(sinks_ref, x_ref, xres_ref, g_ref, win_ref, wdw_ref, bdw_ref, lng_ref, lnb_ref, wout_ref,
              o_ref, q0, k0, v0, z0, q1, k1, v1, z1, mix0, mix1, h_ref, raw_ref, s_ref, p_ref, y_ref,
              bias_ref, mask_ref, *, layer, steps_per_seq):
    s = pl.program_id(0)
    sets = ((q0, k0, v0, z0), (q1, k1, v1, z1))
    mixes = (mix0, mix1)

    @pl.when(s == 0)
    def _():
        t_loc = lax.broadcasted_iota(jnp.int32, (WINDOW, 2 * WINDOW), 0)
        s_loc = lax.broadcasted_iota(jnp.int32, (WINDOW, 2 * WINDOW), 1)
        dist = t_loc + WINDOW - s_loc
        dist_f = jnp.abs(dist).astype(F32)
        in_window = (dist >= 0) & (dist < WINDOW)
        mask_ref[0] = jnp.where(in_window, 1.0, 0.0)
        mask_ref[1] = jnp.where(in_window & (s_loc >= WINDOW), 1.0, 0.0)
        for head in range(N_HEADS):
            slope = 2.0 ** (-8.0 * (head + 1) / N_HEADS)
            bias_ref[head] = (-slope * LOG2E) * dist_f
        for ref in sets[1] + (mix0,):
            ref[...] = jnp.zeros(ref.shape, ref.dtype)

    seq_start = s % steps_per_seq == 0
    for half in range(2):
        new, old = sets[half], sets[1 - half]
        first_new = seq_start if half == 0 else False
        first_old = 0 if half == 0 else seq_start.astype(jnp.int32)
        _interleave([
            (_stage1(x_ref, half * MIX_ROWS, g_ref, win_ref, h_ref, raw_ref, new, old, first_new), 11),
            (_stage2_attention(sinks_ref, old, mixes[1 - half], s_ref, p_ref, bias_ref, mask_ref, 0,
                               first_old, layer=layer), 8),
            (_stage2_attention(sinks_ref, old, mixes[1 - half], s_ref, p_ref, bias_ref, mask_ref, 1,
                               first_old, layer=layer), 8),
            (_stage2_conv(wdw_ref, bdw_ref, lng_ref, lnb_ref, old, y_ref, mixes[1 - half]), 12),
            (_stage3(xres_ref, half * MIX_ROWS, wout_ref, mixes[half], o_ref), 4),
        ])


def _mix_call(x2d, sinks, gain, w_in, w_dw, b_dw, ln_g, ln_b, w_out, layer, seq):
    rows = 2 * MIX_ROWS
    n_steps = x2d.shape[0] // rows
    smem_spec = pl.BlockSpec(memory_space=pltpu.SMEM)

    def vec_spec(width):
        return pl.BlockSpec((None, 1, width), lambda s: (layer, 0, 0))

    new_spec = pl.BlockSpec((rows, D_MODEL), lambda s: (jnp.minimum(s, n_steps - 1), 0))
    old_spec = pl.BlockSpec((rows, D_MODEL), lambda s: (jnp.maximum(s - 1, 0), 0))
    kv_buf = pltpu.VMEM((4, WINDOW + MIX_ROWS, KV_WIDTH), BF16)
    q_buf = pltpu.VMEM((MIX_ROWS, ATTN_WIDTH), BF16)
    z_buf = pltpu.VMEM((Z_HALO_ROWS + Z_BODY_ROWS, LANES), F32)
    mix_buf = pltpu.VMEM((MIX_ROWS, MIX_WIDTH), BF16)
    return pl.pallas_call(
        functools.partial(_mix_body, layer=layer, steps_per_seq=seq // rows),
        grid=(n_steps + 1,),
        in_specs=[
            smem_spec,
            new_spec,
            old_spec,
            vec_spec(D_MODEL),
            pl.BlockSpec((None, D_MODEL, IN_WIDTH), lambda s: (layer, 0, 0)),
            pl.BlockSpec((None, CONV_WIDTH, CONV_CH), lambda s: (layer, 0, 0)),
            vec_spec(CONV_CH),
            vec_spec(CONV_CH),
            vec_spec(CONV_CH),
            pl.BlockSpec((None, MIX_WIDTH, D_MODEL), lambda s: (layer, 0, 0)),
        ],
        out_specs=old_spec,
        out_shape=jax.ShapeDtypeStruct(x2d.shape, F32),
        scratch_shapes=[
            q_buf, kv_buf, kv_buf, z_buf,
            q_buf, kv_buf, kv_buf, z_buf,
            mix_buf, mix_buf,
            pltpu.VMEM((MIX_ROWS, D_MODEL), BF16),
            pltpu.VMEM((MIX_ROWS, IN_WIDTH), F32),
            pltpu.VMEM((4, 2 * WINDOW, 4 * WINDOW), F32),
            pltpu.VMEM((4, 2 * WINDOW, 4 * WINDOW), BF16),
            pltpu.VMEM((CONV_ROW_CHUNK, CONV_CH), F32),
            pltpu.VMEM((N_HEADS, WINDOW, 2 * WINDOW), F32),
            pltpu.VMEM((2, WINDOW, 2 * WINDOW), F32),
        ],
        compiler_params=pltpu.CompilerParams(
            dimension_semantics=("arbitrary",), vmem_limit_bytes=VMEM_LIMIT_BYTES),
        name="mix",
    )(sinks, x2d, x2d, gain, w_in, w_dw, b_dw, ln_g, ln_b, w_out)


def kernel(x, norm_ffn1, w_ffn1_in, w_ffn1_out, norm_mix, w_in, sinks, w_dw, b_dw, conv_ln_g,
           conv_ln_b, w_out, norm_ffn2, w_ffn2_in, w_ffn2_out, final_norm):
    batch, seq, d = x.shape
    assert (d, seq % (2 * MIX_ROWS), (batch * seq) % FFN_ROWS) == (D_MODEL, 0, 0)

    def row3(v):
        return v.reshape(v.shape[0], 1, v.shape[1])

    w1i, w1o = w_ffn1_in.astype(BF16), w_ffn1_out.astype(BF16)
    w2i, w2o = w_ffn2_in.astype(BF16), w_ffn2_out.astype(BF16)
    wi, wo = w_in.astype(BF16), w_out.astype(BF16)
    g1, gm, g2 = row3(norm_ffn1), row3(norm_mix), row3(norm_ffn2)
    bdw, lng, lnb = row3(b_dw), row3(conv_ln_g), row3(conv_ln_b)
    fg = final_norm.reshape(1, D_MODEL)

    x = x.reshape(batch * seq, d)
    for l in range(DEPTH):
        x = _ffn_call(x, g1, w1i, w1o, fg, l, False)
        x = _mix_call(x, sinks, gm, wi, w_dw, bdw, lng, lnb, wo, l, seq)
        x = _ffn_call(x, g2, w2i, w2o, fg, l, l == DEPTH - 1)
    return x.reshape(batch, seq, d)
```

```python
import functools
import math

import jax
import jax.numpy as jnp
from jax import lax
from jax.experimental import pallas as pl
from jax.experimental.pallas import tpu as pltpu

D_MODEL = 1024
DEPTH = 4
HEAD_DIM = 64
N_HEADS = 8
N_KV_HEADS = 2
WINDOW = 128
ATTN_WIDTH = N_HEADS * HEAD_DIM
KV_WIDTH = N_KV_HEADS * HEAD_DIM
CONV_CH = D_MODEL // 2
CONV_WIDTH = 31
MIX_WIDTH = ATTN_WIDTH + CONV_CH
IN_WIDTH = ATTN_WIDTH + 2 * KV_WIDTH + 2 * CONV_CH
D_FF = 2816
FFN_RESIDUAL_WEIGHT = 0.5
EPS = 1e-6
NEG_INF = -1e30
LOG2E = math.log2(math.e)

F32 = jnp.float32
BF16 = jnp.bfloat16

LANES = 128
SUBLANES = 8
MXU_COLS = 256
VMEM_LIMIT_BYTES = 56 * 1024 * 1024

FFN_ROWS = 1024
FFN_COL_CHUNK = MXU_COLS
MIX_ROWS = 256
CONV_HALO = 32
CONV_ROW_CHUNK = 64
CONV_PIECE_ROWS = 32
Z_CHUNKS = CONV_CH // LANES
Z_HALO_ROWS = Z_CHUNKS * CONV_HALO
Z_BODY_ROWS = Z_CHUNKS * MIX_ROWS


def _rms(x, g):
    return x * lax.rsqrt(jnp.mean(x * x, axis=-1, keepdims=True) + EPS) * g


def _sigmoid(x):
    return 1.0 / (1.0 + jnp.exp(-x))


def _ffn_body(x_ref, g_ref, win_ref, wout_ref, fg_ref, o_ref, h_ref, a_ref, *, final):
    h_ref[...] = _rms(x_ref[...], g_ref[...]).astype(BF16)
    for c in range(D_FF // FFN_COL_CHUNK):
        lo = c * FFN_COL_CHUNK
        h = h_ref[...]
        gate = jnp.dot(h, win_ref[:, lo:lo + FFN_COL_CHUNK], preferred_element_type=F32)
        up = jnp.dot(h, win_ref[:, D_FF + lo:D_FF + lo + FFN_COL_CHUNK], preferred_element_type=F32)
        a_ref[:, lo:lo + FFN_COL_CHUNK] = (gate * _sigmoid(gate) * up).astype(BF16)
    for c in range(D_MODEL // MXU_COLS):
        lo = c * MXU_COLS
        y = jnp.dot(a_ref[...], wout_ref[:, lo:lo + MXU_COLS], preferred_element_type=F32)
        o_ref[:, lo:lo + MXU_COLS] = x_ref[:, lo:lo + MXU_COLS] + FFN_RESIDUAL_WEIGHT * y
    if final:
        o_ref[...] = _rms(o_ref[...], fg_ref[...])


def _ffn_call(x2d, gain, w_in, w_out, final_gain, layer, final):
    n_rows = x2d.shape[0]
    vec_spec = pl.BlockSpec((None, 1, D_MODEL), lambda i: (layer, 0, 0))
    return pl.pallas_call(
        functools.partial(_ffn_body, final=final),
        grid=(n_rows // FFN_ROWS,),
        in_specs=[
            pl.BlockSpec((FFN_ROWS, D_MODEL), lambda i: (i, 0)),
            vec_spec,
            pl.BlockSpec((None, D_MODEL, 2 * D_FF), lambda i: (layer, 0, 0),
                         pipeline_mode=pl.Buffered(1)),
            pl.BlockSpec((None, D_FF, D_MODEL), lambda i: (layer, 0, 0),
                         pipeline_mode=pl.Buffered(1)),
            pl.BlockSpec((1, D_MODEL), lambda i: (0, 0)),
        ],
        out_specs=pl.BlockSpec((FFN_ROWS, D_MODEL), lambda i: (i, 0)),
        out_shape=jax.ShapeDtypeStruct(x2d.shape, F32),
        scratch_shapes=[
            pltpu.VMEM((FFN_ROWS, D_MODEL), BF16),
            pltpu.VMEM((FFN_ROWS, D_FF), BF16),
        ],
        compiler_params=pltpu.CompilerParams(
            dimension_semantics=("arbitrary",), vmem_limit_bytes=VMEM_LIMIT_BYTES),
        name="ffn_final" if final else "ffn",
    )(x2d, gain, w_in, w_out, final_gain)


def _interleave(threads):
    gens = [g for g, _ in threads]
    total = [n for _, n in threads]
    done = [0] * len(gens)
    while True:
        cand = [j for j in range(len(gens)) if done[j] < total[j]]
        if not cand:
            break
        i = min(cand, key=lambda j: (done[j] / total[j], j))
        next(gens[i])
        done[i] += 1


def _stage1(x_ref, r0, g_ref, win_ref, h_ref, raw_ref, dst, src, first):
    q_dst, k_dst, v_dst, z_dst = dst
    _, k_src, v_src, z_src = src

    def halo(tail):
        return tail if first is False else jnp.where(first, jnp.zeros_like(tail), tail)

    for d, s in ((k_dst, k_src), (v_dst, v_src)):
        d[:, 0:WINDOW, :] = halo(s[:, MIX_ROWS:MIX_ROWS + WINDOW, :])
    z_dst[0:Z_HALO_ROWS, :] = halo(z_src[Z_BODY_ROWS:Z_BODY_ROWS + Z_HALO_ROWS, :])
    h_ref[...] = _rms(x_ref[r0:r0 + MIX_ROWS, :], g_ref[...]).astype(BF16)
    yield
    for c in range(IN_WIDTH // MXU_COLS):
        lo = c * MXU_COLS
        raw_ref[:, lo:lo + MXU_COLS] = jnp.dot(h_ref[...], win_ref[:, lo:lo + MXU_COLS],
                                               preferred_element_type=F32)
        yield
    q_dst[...] = (raw_ref[:, 0:ATTN_WIDTH] * (LOG2E * HEAD_DIM ** -0.5)).astype(BF16)
    lane = lax.broadcasted_iota(jnp.int32, (MIX_ROWS, KV_WIDTH), 1)
    low = lane < HEAD_DIM
    for off, d in ((ATTN_WIDTH, k_dst), (ATTN_WIDTH + KV_WIDTH, v_dst)):
        t = raw_ref[:, off:off + KV_WIDTH]
        t_sw = pltpu.roll(t, HEAD_DIM, axis=1)
        zero = jnp.zeros_like(t)
        d[0, WINDOW:, :] = jnp.where(low, t, zero).astype(BF16)
        d[1, WINDOW:, :] = jnp.where(low, zero, t_sw).astype(BF16)
        d[2, WINDOW:, :] = jnp.where(low, t_sw, zero).astype(BF16)
        d[3, WINDOW:, :] = jnp.where(low, zero, t).astype(BF16)
    yield
    u_lo = ATTN_WIDTH + 2 * KV_WIDTH
    for c in range(Z_CHUNKS):
        lo = u_lo + c * LANES
        z = raw_ref[:, lo:lo + LANES] * _sigmoid(raw_ref[:, lo + CONV_CH:lo + CONV_CH + LANES])
        for i in range(MIX_ROWS // SUBLANES):
            row = Z_HALO_ROWS + Z_CHUNKS * SUBLANES * i + c
            z_dst[pl.ds(row, SUBLANES, stride=Z_CHUNKS), :] = z[SUBLANES * i:SUBLANES * (i + 1), :]
        if c % 2 == 1:
            yield


def _stage2_attention(sinks_ref, src, mix_ref, s_ref, p_ref, bias_ref, mask_ref, n, first, *, layer):
    q_ref, k_ref, v_ref, _ = src
    r0 = n * WINDOW
    mask_idx = first if n == 0 else 0
    for kvh in range(N_KV_HEADS):
        c0 = kvh * 4 * HEAD_DIM
        slot = 2 * n + kvh
        qg = jnp.concatenate([q_ref[r0:r0 + WINDOW, c0:c0 + 2 * HEAD_DIM],
                              q_ref[r0:r0 + WINDOW, c0 + 2 * HEAD_DIM:c0 + 4 * HEAD_DIM]], axis=0)
        for p in range(2):
            s_ref[slot, :, 2 * WINDOW * p:2 * WINDOW * (p + 1)] = lax.dot_general(
                qg, k_ref[2 * kvh + p, r0:r0 + 2 * WINDOW, :],
                (((1,), (1,)), ((), ())), preferred_element_type=F32)
        yield
        for p in range(2):
            valid = mask_ref[mask_idx] > 0.5
            for half in range(2):
                head = 4 * kvh + 2 * half + p
                sink = sinks_ref[layer, head] * LOG2E
                rows = slice(half * WINDOW, (half + 1) * WINDOW)
                cols = slice(2 * WINDOW * p, 2 * WINDOW * (p + 1))
                sh = jnp.where(valid, s_ref[slot, rows, cols] + bias_ref[head], NEG_INF)
                m = jnp.maximum(jnp.max(sh, axis=-1, keepdims=True), sink)
                e = jnp.exp2(sh - m)
                denom = jnp.sum(e, axis=-1, keepdims=True) + jnp.exp2(sink - m)
                p_ref[slot, rows, cols] = (e * (1.0 / denom)).astype(BF16)
            yield
        vcat = jnp.concatenate([v_ref[2 * kvh, r0:r0 + 2 * WINDOW, :],
                                v_ref[2 * kvh + 1, r0:r0 + 2 * WINDOW, :]], axis=0)
        o = jnp.dot(p_ref[slot], vcat, preferred_element_type=F32)
        mix_ref[r0:r0 + WINDOW, c0:c0 + 2 * HEAD_DIM] = o[0:WINDOW].astype(BF16)
        mix_ref[r0:r0 + WINDOW, c0 + 2 * HEAD_DIM:c0 + 4 * HEAD_DIM] = o[WINDOW:].astype(BF16)
        yield


def _stage2_conv(wdw_ref, bdw_ref, lng_ref, lnb_ref, src, y_ref, mix_ref):
    z_ref = src[3]
    tap0 = CONV_HALO - (CONV_WIDTH - 1)
    for r in range(MIX_ROWS // CONV_ROW_CHUNK):
        r0 = r * CONV_ROW_CHUNK
        for part in range(CONV_ROW_CHUNK // CONV_PIECE_ROWS):
            p0 = part * CONV_PIECE_ROWS
            tiles = CONV_PIECE_ROWS // SUBLANES
            acc = [[None] * Z_CHUNKS for _ in range(tiles)]
            for j in range(CONV_WIDTH):
                w = [wdw_ref[j:j + 1, LANES * c:LANES * (c + 1)] for c in range(Z_CHUNKS)]
                for g in range(tiles):
                    for c in range(Z_CHUNKS):
                        row = Z_CHUNKS * (r0 + p0 + SUBLANES * g + tap0 + j) + c
                        term = w[c] * z_ref[pl.ds(row, SUBLANES, stride=Z_CHUNKS), :]
                        acc[g][c] = term if acc[g][c] is None else acc[g][c] + term
            y_ref[p0:p0 + CONV_PIECE_ROWS, :] = jnp.concatenate(
                [jnp.concatenate(a, axis=1) for a in acc], axis=0)
            yield
        y = y_ref[...] + bdw_ref[...]
        mu = jnp.mean(y, axis=-1, keepdims=True)
        var = jnp.mean(jnp.square(y - mu), axis=-1, keepdims=True)
        y = (y - mu) * lax.rsqrt(var + EPS) * lng_ref[...] + lnb_ref[...]
        mix_ref[r0:r0 + CONV_ROW_CHUNK, ATTN_WIDTH:] = (y * _sigmoid(y)).astype(BF16)
        yield


def _stage3(xres_ref, r0, wout_ref, mix_ref, o_ref):
    for c in range(D_MODEL // MXU_COLS):
        lo = c * MXU_COLS
        y = jnp.dot(mix_ref[...], wout_ref[:, lo:lo + MXU_COLS], preferred_element_type=F32)
        o_ref[r0:r0 + MIX_ROWS, lo:lo + MXU_COLS] = xres_ref[r0:r0 + MIX_ROWS, lo:lo + MXU_COLS] + y
        yield


def _mix_body(sinks_ref, x_ref, xres_ref, g_ref, win_ref, wdw_ref, bdw_ref, lng_ref, lnb_ref, wout_ref,
              o_ref, q0, k0, v0, z0, q1, k1, v1, z1, mix0, mix1, h_ref, raw_ref, s_ref, p_ref, y_ref,
              bias_ref, mask_ref, *, layer, steps_per_seq):
    s = pl.program_id(0)
    sets = ((q0, k0, v0, z0), (q1, k1, v1, z1))
    mixes = (mix0, mix1)

    @pl.when(s == 0)
    def _():
        t_loc = lax.broadcasted_iota(jnp.int32, (WINDOW, 2 * WINDOW), 0)
        s_loc = lax.broadcasted_iota(jnp.int32, (WINDOW, 2 * WINDOW), 1)
        dist = t_loc + WINDOW - s_loc
        dist_f = jnp.abs(dist).astype(F32)
        in_window = (dist >= 0) & (dist < WINDOW)
        mask_ref[0] = jnp.where(in_window, 1.0, 0.0)
        mask_ref[1] = jnp.where(in_window & (s_loc >= WINDOW), 1.0, 0.0)
        for head in range(N_HEADS):
            slope = 2.0 ** (-8.0 * (head + 1) / N_HEADS)
            bias_ref[head] = (-slope * LOG2E) * dist_f
        for ref in sets[1] + (mix0,):
            ref[...] = jnp.zeros(ref.shape, ref.dtype)

    seq_start = s % steps_per_seq == 0
    for half in range(2):
        new, old = sets[half], sets[1 - half]
        first_new = seq_start if half == 0 else False
        first_old = 0 if half == 0 else seq_start.astype(jnp.int32)
        _interleave([
            (_stage1(x_ref, half * MIX_ROWS, g_ref, win_ref, h_ref, raw_ref, new, old, first_new), 11),
            (_stage2_attention(sinks_ref, old, mixes[1 - half], s_ref, p_ref, bias_ref, mask_ref, 0,
                               first_old, layer=layer), 8),
            (_stage2_attention(sinks_ref, old, mixes[1 - half], s_ref, p_ref, bias_ref, mask_ref, 1,
                               first_old, layer=layer), 8),
            (_stage2_conv(wdw_ref, bdw_ref, lng_ref, lnb_ref, old, y_ref, mixes[1 - half]), 12),
            (_stage3(xres_ref, half * MIX_ROWS, wout_ref, mixes[half], o_ref), 4),
        ])


def _mix_call(x2d, sinks, gain, w_in, w_dw, b_dw, ln_g, ln_b, w_out, layer, seq):
    rows = 2 * MIX_ROWS
    n_steps = x2d.shape[0] // rows
    smem_spec = pl.BlockSpec(memory_space=pltpu.SMEM)

    def vec_spec(width):
        return pl.BlockSpec((None, 1, width), lambda s: (layer, 0, 0))

    new_spec = pl.BlockSpec((rows, D_MODEL), lambda s: (jnp.minimum(s, n_steps - 1), 0))
    old_spec = pl.BlockSpec((rows, D_MODEL), lambda s: (jnp.maximum(s - 1, 0), 0))
    kv_buf = pltpu.VMEM((4, WINDOW + MIX_ROWS, KV_WIDTH), BF16)
    q_buf = pltpu.VMEM((MIX_ROWS, ATTN_WIDTH), BF16)
    z_buf = pltpu.VMEM((Z_HALO_ROWS + Z_BODY_ROWS, LANES), F32)
    mix_buf = pltpu.VMEM((MIX_ROWS, MIX_WIDTH), BF16)
    return pl.pallas_call(
        functools.partial(_mix_body, layer=layer, steps_per_seq=seq // rows),
        grid=(n_steps + 1,),
        in_specs=[
            smem_spec,
            new_spec,
            old_spec,
            vec_spec(D_MODEL),
            pl.BlockSpec((None, D_MODEL, IN_WIDTH), lambda s: (layer, 0, 0)),
            pl.BlockSpec((None, CONV_WIDTH, CONV_CH), lambda s: (layer, 0, 0)),
            vec_spec(CONV_CH),
            vec_spec(CONV_CH),
            vec_spec(CONV_CH),
            pl.BlockSpec((None, MIX_WIDTH, D_MODEL), lambda s: (layer, 0, 0)),
        ],
        out_specs=old_spec,
        out_shape=jax.ShapeDtypeStruct(x2d.shape, F32),
        scratch_shapes=[
            q_buf, kv_buf, kv_buf, z_buf,
            q_buf, kv_buf, kv_buf, z_buf,
            mix_buf, mix_buf,
            pltpu.VMEM((MIX_ROWS, D_MODEL), BF16),
            pltpu.VMEM((MIX_ROWS, IN_WIDTH), F32),
            pltpu.VMEM((4, 2 * WINDOW, 4 * WINDOW), F32),
            pltpu.VMEM((4, 2 * WINDOW, 4 * WINDOW), BF16),
            pltpu.VMEM((CONV_ROW_CHUNK, CONV_CH), F32),
            pltpu.VMEM((N_HEADS, WINDOW, 2 * WINDOW), F32),
            pltpu.VMEM((2, WINDOW, 2 * WINDOW), F32),
        ],
        compiler_params=pltpu.CompilerParams(
            dimension_semantics=("arbitrary",), vmem_limit_bytes=VMEM_LIMIT_BYTES),
        name="mix",
    )(sinks, x2d, x2d, gain, w_in, w_dw, b_dw, ln_g, ln_b, w_out)


def kernel(x, norm_ffn1, w_ffn1_in, w_ffn1_out, norm_mix, w_in, sinks, w_dw, b_dw, conv_ln_g,
           conv_ln_b, w_out, norm_ffn2, w_ffn2_in, w_ffn2_out, final_norm):
    batch, seq, d = x.shape
    assert (d, seq % (2 * MIX_ROWS), (batch * seq) % FFN_ROWS) == (D_MODEL, 0, 0)

    def row3(v):
        return v.reshape(v.shape[0], 1, v.shape[1])

    w1i, w1o = w_ffn1_in.astype(BF16), w_ffn1_out.astype(BF16)
    w2i, w2o = w_ffn2_in.astype(BF16), w_ffn2_out.astype(BF16)
    wi, wo = w_in.astype(BF16), w_out.astype(BF16)
    g1, gm, g2 = row3(norm_ffn1), row3(norm_mix), row3(norm_ffn2)
    bdw, lng, lnb = row3(b_dw), row3(conv_ln_g), row3(conv_ln_b)
    fg = final_norm.reshape(1, D_MODEL)

    x = x.reshape(batch * seq, d)
    for l in range(DEPTH):
        x = _ffn_call(x, g1, w1i, w1o, fg, l, False)
        x = _mix_call(x, sinks, gm, wi, w_dw, bdw, lng, lnb, wo, l, seq)
        x = _ffn_call(x, g2, w2i, w2o, fg, l, l == DEPTH - 1)
    return x.reshape(batch, seq, d)
```
